```python
import jax, jax.numpy as jnp
from jax import lax
import numpy as np

D_MODEL = 2048
BATCH = 8
SEQ = 2048
DEPTH = 2

PLE_DIM = 256
LN_EPS = 1e-5
CONV_CH = D_MODEL // 2
CONV_WIDTH = 31
ATTN_HEADS = 8
ATTN_HEAD_DIM = 128
ATTN_WIDTH = ATTN_HEADS * ATTN_HEAD_DIM
ROPE_DIM = ATTN_HEAD_DIM // 4
ROPE_THETA = 500000.0
MOBA_BLOCK = 256
MOBA_TOPK = 3
MOBA_QUERY_CHUNK = 16
MLSTM_HEADS = 4
MLSTM_WIDTH = D_MODEL // 2
MLSTM_HEAD_DIM = MLSTM_WIDTH // MLSTM_HEADS
MLSTM_QK_CONV = 4
MLSTM_CHUNK = 64
FFN_HIDDEN = -(-8 * D_MODEL // (3 * 256)) * 256
DN_ALPHA = (2 * DEPTH) ** 0.25
DN_BETA = (8 * DEPTH) ** -0.25
IN_SIZES = (CONV_CH, CONV_CH, ATTN_WIDTH, ATTN_WIDTH, ATTN_WIDTH,
            2 * MLSTM_WIDTH, MLSTM_WIDTH, 2 * MLSTM_HEADS, MLSTM_WIDTH, 3 * D_MODEL)
IN_WIDTH = sum(IN_SIZES)

kernel_name = "hybrid_conv_moba_mlstm_deepnorm"


def layer_norm(x, g, b):
    xf = x.astype(jnp.float32)
    mu = jnp.mean(xf, -1, keepdims=True)
    var = jnp.mean(jnp.square(xf - mu), -1, keepdims=True)
    y = (xf - mu) * lax.rsqrt(var + LN_EPS)
    return (y * g.astype(jnp.float32) + b.astype(jnp.float32)).astype(x.dtype)


def causal_depthwise_conv(x, w, b):
    width, ch = w.shape
    y = lax.conv_general_dilated(x, w[:, None, :].astype(x.dtype), window_strides=(1,),
                                 padding=[(width - 1, 0)],
                                 dimension_numbers=('NWC', 'WIO', 'NWC'),
                                 feature_group_count=ch)
    return y + b


def partial_rotary(x, positions):
    half = ROPE_DIM // 2
    inv_freq = ROPE_THETA ** (-jnp.arange(half, dtype=jnp.float32) / half)
    ang = positions.astype(jnp.float32)[..., None] * inv_freq
    cos = jnp.cos(ang)[:, :, None, :]
    sin = jnp.sin(ang)[:, :, None, :]
    xr = x[..., :ROPE_DIM].astype(jnp.float32)
    x1, x2 = xr[..., :half], xr[..., half:]
    rot = jnp.concatenate([x1 * cos - x2 * sin, x2 * cos + x1 * sin], -1).astype(x.dtype)
    return jnp.concatenate([rot, x[..., ROPE_DIM:]], -1)


def moba_attention(q, k, v):
    B, H, S, Dh = q.shape
    nb = -(-S // MOBA_BLOCK)
    s_pad = nb * MOBA_BLOCK
    pad = [(0, 0), (0, 0), (0, s_pad - S), (0, 0)]
    q = jnp.pad(q * (Dh ** -0.5), pad)
    k = jnp.pad(k, pad)
    v = jnp.pad(v, pad)
    kb = k.reshape(B, H, nb, MOBA_BLOCK, Dh)
    vb = v.reshape(B, H, nb, MOBA_BLOCK, Dh)
    k_mean = jnp.mean(kb.astype(jnp.float32), axis=3).astype(k.dtype)
    n_sel = min(MOBA_TOPK, nb - 1)
    n_chunks = s_pad // MOBA_QUERY_CHUNK
    qc_all = jnp.moveaxis(q.reshape(B, H, n_chunks, MOBA_QUERY_CHUNK, Dh), 2, 0)
    bidx = jnp.arange(B)[:, None, None, None]
    hidx = jnp.arange(H)[None, :, None, None]

    def one_chunk(args):
        q_c, c = args
        start = c * MOBA_QUERY_CHUNK
        blk = start // MOBA_BLOCK
        q_pos = start + jnp.arange(MOBA_QUERY_CHUNK)
        k_pos = blk * MOBA_BLOCK + jnp.arange(MOBA_BLOCK)
        k_own = lax.dynamic_index_in_dim(kb, blk, axis=2, keepdims=False)
        v_own = lax.dynamic_index_in_dim(vb, blk, axis=2, keepdims=False)
        s_own = jnp.einsum('bhqd,bhkd->bhqk', q_c, k_own).astype(jnp.float32)
        s_own = jnp.where(k_pos[None, :] <= q_pos[:, None], s_own, -jnp.inf)
        if n_sel == 0:
            p_own = jax.nn.softmax(s_own, axis=-1).astype(v.dtype)
            return jnp.einsum('bhqk,bhkd->bhqd', p_own, v_own)
        gate = jnp.einsum('bhqd,bhnd->bhqn', q_c, k_mean).astype(jnp.float32)
        gate = jnp.where(jnp.arange(nb) < blk, gate, -jnp.inf)
        _, sel = lax.top_k(gate, n_sel)
        sel_ok = jnp.arange(n_sel) < blk
        k_sel = kb[bidx, hidx, sel]
        v_sel = vb[bidx, hidx, sel]
        s_sel = jnp.einsum('bhqd,bhqnkd->bhqnk', q_c, k_sel).astype(jnp.float32)
        s_sel = jnp.where(sel_ok[:, None], s_sel, -jnp.inf)
        Bq, Hq, QC = s_sel.shape[:3]
        s_all = jnp.concatenate([s_sel.reshape(Bq, Hq, QC, n_sel * MOBA_BLOCK), s_own], -1)
        p = jax.nn.softmax(s_all, axis=-1).astype(v.dtype)
        p_sel = p[..., :n_sel * MOBA_BLOCK].reshape(Bq, Hq, QC, n_sel, MOBA_BLOCK)
        p_own = p[..., n_sel * MOBA_BLOCK:]
        return (jnp.einsum('bhqnk,bhqnkd->bhqd', p_sel, v_sel)
                + jnp.einsum('bhqk,bhkd->bhqd', p_own, v_own))

    outs = lax.map(one_chunk, (qc_all, jnp.arange(n_chunks)))
    return jnp.moveaxis(outs, 0, 2).reshape(B, H, s_pad, Dh)[:, :, :S]


def mlstm_chunkwise(q, k, v, i_pre, f_pre):
    B, H, S, Dh = q.shape
    L = MLSTM_CHUNK
    nc = S // L
    f32 = jnp.float32
    k = k * (Dh ** -0.5)
    logf = jax.nn.log_sigmoid(f_pre.astype(f32))
    ig = i_pre.astype(f32)

    def chunks(a):
        return jnp.moveaxis(a.reshape(B, H, nc, L, *a.shape[3:]), 2, 0)

    causal = jnp.tril(jnp.ones((L, L), dtype=bool))

    def step(carry, xs):
        C, n, m = carry
        qc, kc, vc, lf, ic = xs
        qf, kf, vf = qc.astype(f32), kc.astype(f32), vc.astype(f32)
        b = jnp.cumsum(lf, axis=-1)
        log_d = jnp.where(causal, b[..., :, None] - b[..., None, :] + ic[..., None, :], -jnp.inf)
        log_inter = b + m[..., None]
        m_t = jnp.maximum(log_inter, jnp.max(log_d, -1))
        d = jnp.exp(log_d - m_t[..., None])
        inter = jnp.exp(log_inter - m_t)
        s = jnp.einsum('bhtd,bhsd->bhts', qf, kf) * d
        num = (jnp.einsum('bhts,bhsd->bhtd', s, vf)
               + inter[..., None] * jnp.einsum('bhtk,bhkv->bhtv', qf, C))
        den = jnp.sum(s, -1) + inter * jnp.einsum('bhtk,bhk->bht', qf, n)
        h = num / jnp.maximum(jnp.abs(den), jnp.exp(-m_t))[..., None]
        b_last = b[..., -1]
        log_w = b_last[..., None] - b + ic
        m_new = jnp.maximum(b_last + m, jnp.max(log_w, -1))
        w = jnp.exp(log_w - m_new[..., None])
        decay = jnp.exp(b_last + m - m_new)
        C_new = decay[..., None, None] * C + jnp.einsum('bhs,bhsk,bhsv->bhkv', w, kf, vf)
        n_new = decay[..., None] * n + jnp.einsum('bhs,bhsk->bhk', w, kf)
        return (C_new, n_new, m_new), h

    init = (jnp.zeros((B, H, Dh, Dh), f32), jnp.zeros((B, H, Dh), f32), jnp.zeros((B, H), f32))
    _, hs = lax.scan(step, init, (chunks(q), chunks(k), chunks(v), chunks(logf), chunks(ig)))
    return jnp.moveaxis(hs, 0, 2).reshape(B, H, S, Dh).astype(q.dtype)


def hybrid_mixer(x, positions, w_in, b_in, conv_w, conv_b, conv_ln_g, conv_ln_b,
                 mconv_w, mconv_b, w_br_conv, w_br_attn, w_br_mlstm, w_out):
    B, S, D = x.shape
    z = x @ w_in + b_in
    offs = np.cumsum(IN_SIZES)[:-1].tolist()
    c_val, c_gate, q_a, k_a, v_a, qk_m, v_m, if_m, o_m, gates = jnp.split(z, offs, axis=-1)

    u = c_val * jax.nn.sigmoid(c_gate)
    u = causal_depthwise_conv(u, conv_w, conv_b)
    u = jax.nn.silu(layer_norm(u, conv_ln_g, conv_ln_b))
    y_conv = u @ w_br_conv

    def attn_heads(t):
        return t.reshape(B, S, ATTN_HEADS, ATTN_HEAD_DIM)
    qa = partial_rotary(attn_heads(q_a), positions).transpose(0, 2, 1, 3)
    ka = partial_rotary(attn_heads(k_a), positions).transpose(0, 2, 1, 3)
    va = attn_heads(v_a).transpose(0, 2, 1, 3)
    o = moba_attention(qa, ka, va).transpose(0, 2, 1, 3).reshape(B, S, ATTN_WIDTH)
    y_attn = o @ w_br_attn

    qk = jax.nn.silu(causal_depthwise_conv(qk_m, mconv_w, mconv_b))
    qm, km = jnp.split(qk, 2, axis=-1)
    def m_heads(t):
        return t.reshape(B, S, MLSTM_HEADS, MLSTM_HEAD_DIM).transpose(0, 2, 1, 3)
    i_pre = if_m[..., :MLSTM_HEADS].transpose(0, 2, 1)
    f_pre = if_m[..., MLSTM_HEADS:].transpose(0, 2, 1)
    h = mlstm_chunkwise(m_heads(qm), m_heads(km), m_heads(v_m), i_pre, f_pre)
    h = h.transpose(0, 2, 1, 3).reshape(B, S, MLSTM_WIDTH) * jax.nn.sigmoid(o_m)
    y_mlstm = h @ w_br_mlstm

    g = jax.nn.sigmoid(gates).reshape(B, S, 3, D)
    merged = g[:, :, 0] * y_conv + g[:, :, 1] * y_attn + g[:, :, 2] * y_mlstm
    return merged @ w_out


def setup_inputs(seed: int = 0) -> dict:
    key = jax.random.key(seed)
    ks = jax.random.split(key, 32)
    f32 = jnp.float32

    def nrm(k, shape, fan_in, scale=1.0):
        return jax.random.normal(k, shape, f32) * (scale * fan_in ** -0.5)

    def gain(k, n):
        return 1.0 + 0.02 * jax.random.normal(k, (DEPTH, n), f32)

    def bias(k, n):
        return 0.02 * jax.random.normal(k, (DEPTH, n), f32)

    x = jax.random.normal(ks[0], (BATCH, SEQ, D_MODEL), f32)
    p = jax.random.normal(ks[1], (DEPTH, BATCH, SEQ, PLE_DIM), f32)
    positions = (jax.random.randint(ks[2], (BATCH, 1), 0, 1024, dtype=jnp.int32)
                 + jnp.arange(SEQ, dtype=jnp.int32)[None, :])
    w_in = nrm(ks[3], (DEPTH, D_MODEL, IN_WIDTH), D_MODEL)
    f_off = sum(IN_SIZES[:7]) + MLSTM_HEADS
    b_in = bias(ks[4], IN_WIDTH).at[:, f_off:f_off + MLSTM_HEADS].add(
        jnp.linspace(3.0, 6.0, MLSTM_HEADS, dtype=f32))
    conv_w = nrm(ks[5], (DEPTH, CONV_WIDTH, CONV_CH), CONV_WIDTH)
    conv_b = bias(ks[6], CONV_CH)
    conv_ln_g = gain(ks[7], CONV_CH)
    conv_ln_b = bias(ks[8], CONV_CH)
    mconv_w = nrm(ks[9], (DEPTH, MLSTM_QK_CONV, 2 * MLSTM_WIDTH), MLSTM_QK_CONV)
    mconv_b = bias(ks[10], 2 * MLSTM_WIDTH)
    w_br_conv = nrm(ks[11], (DEPTH, CONV_CH, D_MODEL), CONV_CH)
    w_br_attn = nrm(ks[12], (DEPTH, ATTN_WIDTH, D_MODEL), ATTN_WIDTH)
    w_br_mlstm = nrm(ks[13], (DEPTH, MLSTM_WIDTH, D_MODEL), MLSTM_WIDTH)
    w_out = nrm(ks[14], (DEPTH, D_MODEL, D_MODEL), D_MODEL, DN_BETA)
    ln_mix_g = gain(ks[15], D_MODEL)
    ln_mix_b = bias(ks[16], D_MODEL)
    w_ffn_gate = nrm(ks[17], (DEPTH, D_MODEL, FFN_HIDDEN), D_MODEL)
    w_ffn_up = nrm(ks[18], (DEPTH, D_MODEL, FFN_HIDDEN), D_MODEL)
    w_ffn_down = nrm(ks[19], (DEPTH, FFN_HIDDEN, D_MODEL), FFN_HIDDEN, DN_BETA)
    ln_ffn_g = gain(ks[20], D_MODEL)
    ln_ffn_b = bias(ks[21], D_MODEL)
    w_ple_gate = nrm(ks[22], (DEPTH, D_MODEL, D_MODEL), D_MODEL)
    w_ple_proj = nrm(ks[23], (DEPTH, PLE_DIM, D_MODEL), PLE_DIM, DN_BETA)
    ln_ple_g = gain(ks[24], D_MODEL)
    ln_ple_b = bias(ks[25], D_MODEL)
    return {"x": x, "p": p, "positions": positions, "w_in": w_in, "b_in": b_in,
            "conv_w": conv_w, "conv_b": conv_b, "conv_ln_g": conv_ln_g, "conv_ln_b": conv_ln_b,
            "mconv_w": mconv_w, "mconv_b": mconv_b, "w_br_conv": w_br_conv,
            "w_br_attn": w_br_attn, "w_br_mlstm": w_br_mlstm, "w_out": w_out,
            "ln_mix_g": ln_mix_g, "ln_mix_b": ln_mix_b, "w_ffn_gate": w_ffn_gate,
            "w_ffn_up": w_ffn_up, "w_ffn_down": w_ffn_down, "ln_ffn_g": ln_ffn_g,
            "ln_ffn_b": ln_ffn_b, "w_ple_gate": w_ple_gate, "w_ple_proj": w_ple_proj,
            "ln_ple_g": ln_ple_g, "ln_ple_b": ln_ple_b}


def reference(x, p, positions, w_in, b_in, conv_w, conv_b, conv_ln_g, conv_ln_b,
              mconv_w, mconv_b, w_br_conv, w_br_attn, w_br_mlstm, w_out,
              ln_mix_g, ln_mix_b, w_ffn_gate, w_ffn_up, w_ffn_down, ln_ffn_g, ln_ffn_b,
              w_ple_gate, w_ple_proj, ln_ple_g, ln_ple_b):
    for i in range(DEPTH):
        mix = hybrid_mixer(x, positions, w_in[i], b_in[i], conv_w[i], conv_b[i],
                           conv_ln_g[i], conv_ln_b[i], mconv_w[i], mconv_b[i],
                           w_br_conv[i], w_br_attn[i], w_br_mlstm[i], w_out[i])
        x = layer_norm(DN_ALPHA * x + mix, ln_mix_g[i], ln_mix_b[i])
        ffn = (jax.nn.silu(x @ w_ffn_gate[i]) * (x @ w_ffn_up[i])) @ w_ffn_down[i]
        x = layer_norm(DN_ALPHA * x + ffn, ln_ffn_g[i], ln_ffn_b[i])
        ple = jax.nn.sigmoid(x @ w_ple_gate[i]) * (p[i] @ w_ple_proj[i])
        x = layer_norm(DN_ALPHA * x + ple, ln_ple_g[i], ln_ple_b[i])
    return x
```

```python
import functools

import jax
import jax.numpy as jnp
import numpy as np
from jax import lax
from jax.experimental import pallas as pl
from jax.experimental.pallas import tpu as pltpu

F32 = jnp.float32
BF16 = jnp.bfloat16

LN_EPS = 1e-5
ATTN_HEADS = 8
ATTN_HEAD_DIM = 128
ROPE_DIM = ATTN_HEAD_DIM // 4
ROPE_THETA = 500000.0
MOBA_BLOCK = 256
MOBA_TOPK = 3
MLSTM_HEADS = 4
MLSTM_CHUNK = 256
LANES = 128
CONV_HALO = 32
MCONV_HALO = 8
VMEM_LIMIT_CAP = 58 * 1024 * 1024


def _nbytes(shape, dtype):
    return int(np.prod(shape)) * jnp.dtype(dtype).itemsize


def _params(semantics, block_bytes, extra_bytes=0):
    need = 2 * block_bytes + extra_bytes + (4 << 20)
    return pltpu.CompilerParams(
        dimension_semantics=semantics,
        vmem_limit_bytes=int(min(max(need, 16 << 20), VMEM_LIMIT_CAP)))


def _sigmoid(x):
    return 1.0 / (1.0 + jnp.exp(-x))


def _layer_norm(y, g, b):
    mu = jnp.mean(y, axis=-1, keepdims=True)
    d = y - mu
    var = jnp.mean(d * d, axis=-1, keepdims=True)
    return d * lax.rsqrt(var + LN_EPS) * g + b


def _dot(a, b):
    return jnp.dot(a, b, preferred_element_type=F32)


def _dot_nt(a, b):
    return lax.dot_general(a, b, (((1,), (1,)), ((), ())), preferred_element_type=F32)


def _dot_tn(a, b):
    return lax.dot_general(a, b, (((0,), (0,)), ((), ())), preferred_element_type=F32)


def _proj_kernel(a_ref, w_ref, b_ref, o_ref):
    acc = _dot(a_ref[...], w_ref[...])
    o_ref[...] = (acc + b_ref[...]).astype(o_ref.dtype)


def _proj(a, w, bias, out_dtype, tm, tn):
    m, k = a.shape
    n = w.shape[1]
    blocks = (_nbytes((tm, k), a.dtype) + _nbytes((k, tn), w.dtype)
              + _nbytes((tm, tn), out_dtype) + _nbytes((1, tn), F32))
    return pl.pallas_call(
        _proj_kernel,
        grid=(m // tm, n // tn),
        in_specs=[pl.BlockSpec((tm, k), lambda i, j: (i, 0)),
                  pl.BlockSpec((k, tn), lambda i, j: (0, j)),
                  pl.BlockSpec((1, tn), lambda i, j: (0, j))],
        out_specs=pl.BlockSpec((tm, tn), lambda i, j: (i, j)),
        out_shape=jax.ShapeDtypeStruct((m, n), out_dtype),
        compiler_params=_params(("parallel", "arbitrary"), blocks,
                                _nbytes((tm, tn), F32)),
        name="in_proj",
    )(a, w, bias)


def _rope_table_kernel(pos_ref, inv_ref, c_ref, s_ref):
    ang = pos_ref[...] * inv_ref[...]
    lane = lax.broadcasted_iota(jnp.int32, ang.shape, 1)
    half = ROPE_DIM // 2
    c_ref[...] = jnp.where(lane < ROPE_DIM, jnp.cos(ang), 1.0)
    sn = jnp.sin(ang)
    s_ref[...] = jnp.where(lane < half, -sn, jnp.where(lane < ROPE_DIM, sn, 0.0))


def _rope_tables(pos_f, inv_lane, tm):
    n = pos_f.shape[0]
    return pl.pallas_call(
        _rope_table_kernel,
        grid=(n // tm,),
        in_specs=[pl.BlockSpec((tm, 1), lambda i: (i, 0)),
                  pl.BlockSpec((1, LANES), lambda i: (0, 0))],
        out_specs=[pl.BlockSpec((tm, LANES), lambda i: (i, 0)),
                   pl.BlockSpec((tm, LANES), lambda i: (i, 0))],
        out_shape=[jax.ShapeDtypeStruct((n, LANES), F32),
                   jax.ShapeDtypeStruct((n, LANES), F32)],
        compiler_params=_params(("parallel",), 3 * _nbytes((tm, LANES), F32)),
        name="rope_tables",
    )(pos_f, inv_lane)


def _attn_prep_kernel(zq_ref, zk_ref, zv_ref, c_ref, s_ref,
                      q_ref, k_ref, v_ref, km_ref):
    width = zq_ref.shape[1]
    reps = width // LANES
    half = ROPE_DIM // 2
    c = jnp.concatenate([c_ref[...]] * reps, axis=1)
    s = jnp.concatenate([s_ref[...]] * reps, axis=1)
    lane = lax.broadcasted_iota(jnp.int32, c.shape, 1) % LANES
    low = lane < half

    def rot(x):
        partner = jnp.where(low, pltpu.roll(x, width - half, 1), pltpu.roll(x, half, 1))
        return x * c + partner * s

    q = rot(zq_ref[...].astype(F32))
    k = rot(zk_ref[...].astype(F32))
    q_ref[...] = q * (ATTN_HEAD_DIM ** -0.5)
    k_ref[...] = k.astype(BF16)
    v_ref[...] = zv_ref[...].astype(BF16)
    km_ref[...] = jnp.mean(k, axis=0, keepdims=True)


def _attn_prep(z, col_q, col_k, col_v, c_tab, s_tab, width):
    n = z.shape[0]
    blk = MOBA_BLOCK
    nblk = n // blk
    blocks = (3 * _nbytes((blk, width), z.dtype) + 2 * _nbytes((blk, LANES), F32)
              + _nbytes((blk, width), F32) + 2 * _nbytes((blk, width), BF16))
    return pl.pallas_call(
        _attn_prep_kernel,
        grid=(nblk,),
        in_specs=[pl.BlockSpec((blk, width), lambda i: (i, col_q)),
                  pl.BlockSpec((blk, width), lambda i: (i, col_k)),
                  pl.BlockSpec((blk, width), lambda i: (i, col_v)),
                  pl.BlockSpec((blk, LANES), lambda i: (i, 0)),
                  pl.BlockSpec((blk, LANES), lambda i: (i, 0))],
        out_specs=[pl.BlockSpec((blk, width), lambda i: (i, 0)),
                   pl.BlockSpec((blk, width), lambda i: (i, 0)),
                   pl.BlockSpec((blk, width), lambda i: (i, 0)),
                   pl.BlockSpec((None, 1, width), lambda i: (i, 0, 0))],
        out_shape=[jax.ShapeDtypeStruct((n, width), F32),
                   jax.ShapeDtypeStruct((n, width), BF16),
                   jax.ShapeDtypeStruct((n, width), BF16),
                   jax.ShapeDtypeStruct((nblk, 1, width), F32)],
        compiler_params=_params(("parallel",), blocks, 6 * _nbytes((blk, width), F32)),
        name="attn_prep",
    )(z, z, z, c_tab, s_tab)


def _moba_kernel(q_ref, k_ref, v_ref, km_ref, o_ref, m_sc, l_sc, acc_sc):
    blk = q_ref.shape[0]
    nb = km_ref.shape[0]
    i = pl.program_id(2)
    q = q_ref[...]
    qb = q.astype(BF16)

    q1, q2, q3 = _split3(q)
    k1, k2, k3 = _split3(km_ref[...])
    gate = (_dot_nt(q1, k1) + (_dot_nt(q1, k2) + _dot_nt(q2, k1))
            + (_dot_nt(q1, k3) + _dot_nt(q2, k2) + _dot_nt(q3, k1)))
    bid = lax.broadcasted_iota(jnp.int32, gate.shape, 1)
    valid = bid < i
    g = jnp.where(valid, gate, -jnp.inf)
    rank = jnp.zeros(gate.shape, F32)
    for n in range(nb):
        gn = g[:, n:n + 1]
        beats = jnp.logical_or(gn > g, jnp.logical_and(gn == g, n < bid))
        rank = rank + jnp.where(beats, 1.0, 0.0)
    sel = jnp.where(jnp.logical_and(valid, rank < MOBA_TOPK), 1.0, 0.0)

    start = pl.multiple_of(i * blk, blk)
    k_own = k_ref[pl.ds(start, blk), :]
    v_own = v_ref[pl.ds(start, blk), :]
    s = _dot_nt(qb, k_own)
    row = lax.broadcasted_iota(jnp.int32, s.shape, 0)
    col = lax.broadcasted_iota(jnp.int32, s.shape, 1)
    s = jnp.where(col <= row, s, -jnp.inf)
    m0 = jnp.max(s, axis=1, keepdims=True)
    p = jnp.exp(s - m0)
    m_sc[...] = m0
    l_sc[...] = jnp.sum(p, axis=1, keepdims=True)
    acc_sc[...] = _dot(p.astype(BF16), v_own)

    for j in range(nb - 1):
        @pl.when(j < i)
        def _(j=j):
            kj = k_ref[j * blk:(j + 1) * blk, :]
            vj = v_ref[j * blk:(j + 1) * blk, :]
            sj = _dot_nt(qb, kj)
            sj = jnp.where(sel[:, j:j + 1] > 0.0, sj, -jnp.inf)
            m_old = m_sc[...]
            m_new = jnp.maximum(m_old, jnp.max(sj, axis=1, keepdims=True))
            pj = jnp.exp(sj - m_new)
            a = jnp.exp(m_old - m_new)
            m_sc[...] = m_new
            l_sc[...] = a * l_sc[...] + jnp.sum(pj, axis=1, keepdims=True)
            acc_sc[...] = a * acc_sc[...] + _dot(pj.astype(BF16), vj)

    o_ref[...] = (acc_sc[...] / l_sc[...]).astype(o_ref.dtype)


def _moba(q, k, v, kmean, batch, seq):
    n, width = q.shape
    heads = width // ATTN_HEAD_DIM
    blk = MOBA_BLOCK
    nb = seq // blk
    dh = ATTN_HEAD_DIM
    blocks = (_nbytes((blk, dh), F32) + 2 * _nbytes((seq, dh), BF16)
              + _nbytes((nb, dh), F32) + _nbytes((blk, dh), BF16))
    return pl.pallas_call(
        _moba_kernel,
        grid=(batch, heads, nb),
        in_specs=[pl.BlockSpec((blk, dh), lambda b, h, i: (b * nb + i, h)),
                  pl.BlockSpec((seq, dh), lambda b, h, i: (b, h)),
                  pl.BlockSpec((seq, dh), lambda b, h, i: (b, h)),
                  pl.BlockSpec((None, nb, dh), lambda b, h, i: (b, 0, h))],
        out_specs=pl.BlockSpec((blk, dh), lambda b, h, i: (b * nb + i, h)),
        out_shape=jax.ShapeDtypeStruct((n, width), BF16),
        scratch_shapes=[pltpu.VMEM((blk, 1), F32), pltpu.VMEM((blk, 1), F32),
                        pltpu.VMEM((blk, dh), F32)],
        compiler_params=_params(("parallel", "parallel", "arbitrary"), blocks,
                                8 * _nbytes((blk, blk), F32)),
        name="moba_attention",
    )(q, k, v, kmean)


def _conv_taps(xs_ref, w_ref, out_ref, halo, bias_ref):
    width = w_ref.shape[0]
    rows = out_ref.shape[0]
    base = halo - (width - 1)

    def column(cidx, carry):
        c0 = pl.multiple_of(cidx * LANES, LANES)
        cols = pl.ds(c0, LANES)
        acc = jnp.zeros((rows, LANES), F32) + bias_ref[:, cols]
        for j in range(width):
            acc = acc + xs_ref[pl.ds(base + j, rows), cols] * w_ref[j:j + 1, cols]
        out_ref[:, cols] = acc
        return carry

    lax.fori_loop(0, out_ref.shape[1] // LANES, column, 0)


def _conv_module_kernel(val_ref, gate_ref, hval_ref, hgate_ref, w_ref, b_ref,
                        g_ref, beta_ref, o_ref, xs_sc, y_sc):
    halo = hval_ref.shape[0]
    has_past = pl.program_id(1) > 0
    hv = hval_ref[...].astype(F32)
    hg = hgate_ref[...].astype(F32)
    xs_sc[0:halo, :] = jnp.where(has_past, hv * _sigmoid(hg), 0.0)
    xs_sc[halo:, :] = val_ref[...].astype(F32) * _sigmoid(gate_ref[...].astype(F32))
    _conv_taps(xs_sc, w_ref, y_sc, halo, b_ref)
    u = _layer_norm(y_sc[...], g_ref[...], beta_ref[...])
    o_ref[...] = (u * _sigmoid(u)).astype(o_ref.dtype)


def _conv_module(z, col_val, col_gate, w, b, g, beta, batch, seq, ts):
    n = z.shape[0]
    ch = w.shape[1]
    halo = CONV_HALO
    nt = seq // ts
    hpb = ts // halo
    blocks = (2 * _nbytes((ts, ch), z.dtype) + 2 * _nbytes((halo, ch), z.dtype)
              + _nbytes(w.shape, F32) + 3 * _nbytes((1, ch), F32) + _nbytes((ts, ch), BF16))

    def cur(col):
        return pl.BlockSpec((ts, ch), lambda bi, i: (bi * nt + i, col))

    def past(col):
        return pl.BlockSpec(
            (halo, ch), lambda bi, i: (jnp.maximum((bi * nt + i) * hpb - 1, 0), col))

    vec = pl.BlockSpec((1, ch), lambda bi, i: (0, 0))
    return pl.pallas_call(
        _conv_module_kernel,
        grid=(batch, nt),
        in_specs=[cur(col_val), cur(col_gate), past(col_val), past(col_gate),
                  pl.BlockSpec(w.shape, lambda bi, i: (0, 0)), vec, vec, vec],
        out_specs=pl.BlockSpec((ts, ch), lambda bi, i: (bi * nt + i, 0)),
        out_shape=jax.ShapeDtypeStruct((n, ch), BF16),
        scratch_shapes=[pltpu.VMEM((halo + ts, ch), F32), pltpu.VMEM((ts, ch), F32)],
        compiler_params=_params(("parallel", "arbitrary"), blocks,
                                _nbytes((halo + 5 * ts, ch), F32)),
        name="conv_module",
    )(z, z, z, z, w, b, g, beta)


def _qk_conv_kernel(x_ref, hx_ref, w_ref, b_ref, o_ref, xs_sc, y_sc, *, k_scale):
    halo = hx_ref.shape[0]
    has_past = pl.program_id(1) > 0
    xs_sc[0:halo, :] = jnp.where(has_past, hx_ref[...].astype(F32), 0.0)
    xs_sc[halo:, :] = x_ref[...].astype(F32)
    _conv_taps(xs_sc, w_ref, y_sc, halo, b_ref)
    y = y_sc[...]
    y = y * _sigmoid(y)
    half = y.shape[1] // 2
    lane = lax.broadcasted_iota(jnp.int32, y.shape, 1)
    o_ref[...] = (y * jnp.where(lane < half, 1.0, k_scale)).astype(o_ref.dtype)


def _qk_conv(z, col, w, b, batch, seq, ts, k_scale):
    n = z.shape[0]
    ch = w.shape[1]
    halo = MCONV_HALO
    nt = seq // ts
    hpb = ts // halo
    blocks = (_nbytes((ts, ch), z.dtype) + _nbytes((halo, ch), z.dtype)
              + _nbytes(w.shape, F32) + _nbytes((1, ch), F32) + _nbytes((ts, ch), BF16))
    return pl.pallas_call(
        functools.partial(_qk_conv_kernel, k_scale=k_scale),
        grid=(batch, nt),
        in_specs=[pl.BlockSpec((ts, ch), lambda bi, i: (bi * nt + i, col)),
                  pl.BlockSpec((halo, ch),
                               lambda bi, i: (jnp.maximum((bi * nt + i) * hpb - 1, 0), col)),
                  pl.BlockSpec(w.shape, lambda bi, i: (0, 0)),
                  pl.BlockSpec((1, ch), lambda bi, i: (0, 0))],
        out_specs=pl.BlockSpec((ts, ch), lambda bi, i: (bi * nt + i, 0)),
        out_shape=jax.ShapeDtypeStruct((n, ch), BF16),
        scratch_shapes=[pltpu.VMEM((halo + ts, ch), F32), pltpu.VMEM((ts, ch), F32)],
        compiler_params=_params(("parallel", "arbitrary"), blocks,
                                _nbytes((halo + 4 * ts, ch), F32)),
        name="mlstm_qk_conv",
    )(z, z, w, b)


def _split3(x):
    h1 = x.astype(BF16)
    r1 = x - h1.astype(F32)
    h2 = r1.astype(BF16)
    r2 = r1 - h2.astype(F32)
    return h1, h2, r2.astype(BF16)


def _log_sigmoid(x):
    return jnp.minimum(x, 0.0) - jnp.log(1.0 + jnp.exp(-jnp.abs(x)))


def _mlstm_kernel(qk_ref, v_ref, og_ref, gcol_ref, grow_ref, o_ref,
                  c_sc, n_sc, m_sc):
    chunk = v_ref.shape[0]
    heads = MLSTM_HEADS
    dh = v_ref.shape[1] // heads
    width = v_ref.shape[1]

    @pl.when(pl.program_id(1) == 0)
    def _():
        c_sc[...] = jnp.zeros_like(c_sc)
        n_sc[...] = jnp.zeros_like(n_sc)
        m_sc[...] = jnp.zeros_like(m_sc)

    row = lax.broadcasted_iota(jnp.int32, (chunk, chunk), 0)
    col = lax.broadcasted_iota(jnp.int32, (chunk, chunk), 1)
    causal = col <= row
    tri = jnp.where(causal, 1.0, 0.0).astype(BF16)
    tri_t = jnp.where(row <= col, 1.0, 0.0).astype(BF16)

    gcol = gcol_ref[...]
    grow = grow_ref[...]
    lf_col = _log_sigmoid(gcol)
    lf_row = _log_sigmoid(grow)
    c1, c2, c3 = _split3(lf_col)
    bcol = _dot(tri, c1) + _dot(tri, c2) + _dot(tri, c3)
    r1, r2, r3 = _split3(lf_row)
    brow = _dot(r1, tri_t) + _dot(r2, tri_t) + _dot(r3, tri_t)

    for h in range(heads):
        hs = slice(h * dh, (h + 1) * dh)
        ic_col = gcol[:, h:h + 1]
        b_col = bcol[:, heads + h:heads + h + 1]
        ic_row = grow[h:h + 1, :]
        b_row = brow[heads + h:heads + h + 1, :]
        m = m_sc[h:h + 1, 0:1]
        q = qk_ref[:, h * dh:(h + 1) * dh]
        k = qk_ref[:, width + h * dh:width + (h + 1) * dh]
        v = v_ref[:, hs].astype(BF16)
        c_old = c_sc[h]
        n_old = n_sc[h]

        log_d = jnp.where(causal, b_col + (ic_row - b_row), -jnp.inf)
        log_inter = b_col + m
        m_t = jnp.maximum(log_inter, jnp.max(log_d, axis=1, keepdims=True))
        d = jnp.exp(log_d - m_t)
        inter = jnp.exp(log_inter - m_t)
        s = _dot_nt(q, k) * d
        num = _dot(s.astype(BF16), v) + inter * _dot(q, c_old.astype(BF16))
        qn = jnp.sum(q.astype(F32) * n_old, axis=1, keepdims=True)
        den = jnp.sum(s, axis=1, keepdims=True) + inter * qn
        hid = num / jnp.maximum(jnp.abs(den), jnp.exp(-m_t))
        og = og_ref[:, hs].astype(F32)
        o_ref[:, hs] = (hid * _sigmoid(og)).astype(o_ref.dtype)

        b_last = b_col[chunk - 1:chunk, :]
        log_w = b_last - b_col + ic_col
        m_new = jnp.maximum(b_last + m, jnp.max(log_w, axis=0, keepdims=True))
        w = jnp.exp(log_w - m_new)
        decay = jnp.exp(b_last + m - m_new)
        kw = k.astype(F32) * w
        c_sc[h] = decay * c_old + _dot_tn(kw.astype(BF16), v)
        n_sc[h] = decay * n_old + jnp.sum(kw, axis=0, keepdims=True)
        m_sc[h:h + 1, :] = jnp.broadcast_to(m_new, (1, m_sc.shape[1]))


def _mlstm(qk, z, col_v, col_o, gates_col, gates_row, batch, seq):
    n = qk.shape[0]
    width = qk.shape[1] // 2
    chunk = MLSTM_CHUNK
    nc = seq // chunk
    dh = width // MLSTM_HEADS
    blocks = (_nbytes((chunk, 2 * width), qk.dtype) + 2 * _nbytes((chunk, width), z.dtype)
              + _nbytes((chunk, LANES), F32) + _nbytes((8, chunk), F32)
              + _nbytes((chunk, width), BF16))
    scratch = _nbytes((MLSTM_HEADS, dh, dh), F32)
    return pl.pallas_call(
        _mlstm_kernel,
        grid=(batch, nc),
        in_specs=[pl.BlockSpec((chunk, 2 * width), lambda b, c: (b * nc + c, 0)),
                  pl.BlockSpec((chunk, width), lambda b, c: (b * nc + c, col_v)),
                  pl.BlockSpec((chunk, width), lambda b, c: (b * nc + c, col_o)),
                  pl.BlockSpec((chunk, LANES), lambda b, c: (b * nc + c, 0)),
                  pl.BlockSpec((None, 8, chunk), lambda b, c: (b, 0, c))],
        out_specs=pl.BlockSpec((chunk, width), lambda b, c: (b * nc + c, 0)),
        out_shape=jax.ShapeDtypeStruct((n, width), BF16),
        scratch_shapes=[pltpu.VMEM((MLSTM_HEADS, dh, dh), F32),
                        pltpu.VMEM((MLSTM_HEADS, 1, dh), F32),
                        pltpu.VMEM((8, LANES), F32)],
        compiler_params=_params(("parallel", "arbitrary"), blocks,
                                scratch + 16 * _nbytes((chunk, chunk), F32)),
        name="mlstm_scan",
    )(qk, z, z, gates_col, gates_row)


def _merge_kernel(uc_ref, ua_ref, um_ref, wc_ref, wa_ref, wm_ref,
                  g0_ref, g1_ref, g2_ref, o_ref):
    acc = _sigmoid(g0_ref[...].astype(F32)) * _dot(uc_ref[...], wc_ref[...])
    acc = acc + _sigmoid(g1_ref[...].astype(F32)) * _dot(ua_ref[...], wa_ref[...])
    acc = acc + _sigmoid(g2_ref[...].astype(F32)) * _dot(um_ref[...], wm_ref[...])
    o_ref[...] = acc.astype(o_ref.dtype)


def _merge(uc, ua, um, wc, wa, wm, z, gate_col0, tm, tn):
    m, k = uc.shape
    n = wc.shape[1]
    nj = n // tn
    blocks = (3 * _nbytes((tm, k), BF16) + 3 * _nbytes((k, tn), BF16)
              + 3 * _nbytes((tm, tn), z.dtype) + _nbytes((tm, tn), BF16))
    a_spec = pl.BlockSpec((tm, k), lambda i, j: (i, 0))
    w_spec = pl.BlockSpec((k, tn), lambda i, j: (0, j))

    def gate_spec(branch):
        return pl.BlockSpec((tm, tn), lambda i, j: (i, gate_col0 + branch * nj + j))

    return pl.pallas_call(
        _merge_kernel,
        grid=(m // tm, nj),
        in_specs=[a_spec, a_spec, a_spec, w_spec, w_spec, w_spec,
                  gate_spec(0), gate_spec(1), gate_spec(2)],
        out_specs=pl.BlockSpec((tm, tn), lambda i, j: (i, j)),
        out_shape=jax.ShapeDtypeStruct((m, n), BF16),
        compiler_params=_params(("parallel", "arbitrary"), blocks,
                                4 * _nbytes((tm, tn), F32)),
        name="branch_merge",
    )(uc, ua, um, wc, wa, wm, z, z, z)


def _write_ln(y, g_ref, b_ref, xo_ref, xb_ref):
    out = _layer_norm(y, g_ref[...], b_ref[...])
    xo_ref[...] = out
    xb_ref[...] = out.astype(BF16)


def _out_ln_kernel(a_ref, w_ref, x_ref, g_ref, b_ref, xo_ref, xb_ref, *, alpha):
    y = alpha * x_ref[...] + _dot(a_ref[...], w_ref[...])
    _write_ln(y, g_ref, b_ref, xo_ref, xb_ref)


def _row_outs(m, n, tm, index_map):
    specs = [pl.BlockSpec((tm, n), index_map), pl.BlockSpec((tm, n), index_map)]
    shapes = [jax.ShapeDtypeStruct((m, n), F32), jax.ShapeDtypeStruct((m, n), BF16)]
    return specs, shapes


def _out_ln(a, w, x, g, b, alpha, tm):
    m, k = a.shape
    n = w.shape[1]
    blocks = (_nbytes((tm, k), BF16) + _nbytes((k, n), BF16) + 2 * _nbytes((tm, n), F32)
              + _nbytes((tm, n), BF16) + 2 * _nbytes((1, n), F32))
    out_specs, out_shapes = _row_outs(m, n, tm, lambda i: (i, 0))
    vec = pl.BlockSpec((1, n), lambda i: (0, 0))
    return pl.pallas_call(
        functools.partial(_out_ln_kernel, alpha=alpha),
        grid=(m // tm,),
        in_specs=[pl.BlockSpec((tm, k), lambda i: (i, 0)),
                  pl.BlockSpec((k, n), lambda i: (0, 0)),
                  pl.BlockSpec((tm, n), lambda i: (i, 0)), vec, vec],
        out_specs=out_specs,
        out_shape=out_shapes,
        compiler_params=_params(("parallel",), blocks, 3 * _nbytes((tm, n), F32)),
        name="out_proj_ln",
    )(a, w, x, g, b)


def _ffn_up_kernel(a_ref, wg_ref, wu_ref, o_ref):
    a = a_ref[...]
    gate = _dot(a, wg_ref[...])
    up = _dot(a, wu_ref[...])
    o_ref[...] = (gate * _sigmoid(gate) * up).astype(o_ref.dtype)


def _ffn_up(a, wg, wu, tm, tn):
    m, k = a.shape
    n = wg.shape[1]
    blocks = _nbytes((tm, k), BF16) + 2 * _nbytes((k, tn), BF16) + _nbytes((tm, tn), BF16)
    w_spec = pl.BlockSpec((k, tn), lambda i, j: (0, j))
    return pl.pallas_call(
        _ffn_up_kernel,
        grid=(m // tm, n // tn),
        in_specs=[pl.BlockSpec((tm, k), lambda i, j: (i, 0)), w_spec, w_spec],
        out_specs=pl.BlockSpec((tm, tn), lambda i, j: (i, j)),
        out_shape=jax.ShapeDtypeStruct((m, n), BF16),
        compiler_params=_params(("parallel", "arbitrary"), blocks,
                                4 * _nbytes((tm, tn), F32)),
        name="ffn_up",
    )(a, wg, wu)


def _ffn_down_kernel(a_ref, w_ref, x_ref, g_ref, b_ref, xo_ref, xb_ref, acc_sc, *, alpha):
    kk = pl.program_id(1)

    @pl.when(kk == 0)
    def _():
        acc_sc[...] = alpha * x_ref[...]

    acc_sc[...] += _dot(a_ref[...], w_ref[...])

    @pl.when(kk == pl.num_programs(1) - 1)
    def _():
        _write_ln(acc_sc[...], g_ref, b_ref, xo_ref, xb_ref)


def _ffn_down(a, w, x, g, b, alpha, tm, tk):
    m, k = a.shape
    n = w.shape[1]
    blocks = (_nbytes((tm, tk), BF16) + _nbytes((tk, n), BF16) + 2 * _nbytes((tm, n), F32)
              + _nbytes((tm, n), BF16) + 2 * _nbytes((1, n), F32))
    out_specs, out_shapes = _row_outs(m, n, tm, lambda i, kk: (i, 0))
    vec = pl.BlockSpec((1, n), lambda i, kk: (0, 0))
    return pl.pallas_call(
        functools.partial(_ffn_down_kernel, alpha=alpha),
        grid=(m // tm, k // tk),
        in_specs=[pl.BlockSpec((tm, tk), lambda i, kk: (i, kk)),
                  pl.BlockSpec((tk, n), lambda i, kk: (kk, 0)),
                  pl.BlockSpec((tm, n), lambda i, kk: (i, 0)), vec, vec],
        out_specs=out_specs,
        out_shape=out_shapes,
        scratch_shapes=[pltpu.VMEM((tm, n), F32)],
        compiler_params=_params(("parallel", "arbitrary"), blocks,
                                3 * _nbytes((tm, n), F32)),
        name="ffn_down_ln",
    )(a, w, x, g, b)


def _ple_kernel(xb_ref, wg_ref, p_ref, wp_ref, x_ref, g_ref, b_ref, xo_ref, xb_out_ref,
                *, alpha):
    gate = _sigmoid(_dot(xb_ref[...], wg_ref[...]))
    proj = _dot(p_ref[...].astype(BF16), wp_ref[...])
    y = alpha * x_ref[...] + gate * proj
    _write_ln(y, g_ref, b_ref, xo_ref, xb_out_ref)


def _ple(xb, wg, p, wp, x, g, b, alpha, tm):
    m, k = xb.shape
    n = wg.shape[1]
    kp = p.shape[1]
    blocks = (_nbytes((tm, k), BF16) + _nbytes((k, n), BF16) + _nbytes((tm, kp), F32)
              + _nbytes((kp, n), BF16) + 2 * _nbytes((tm, n), F32) + _nbytes((tm, n), BF16)
              + 2 * _nbytes((1, n), F32))
    out_specs, out_shapes = _row_outs(m, n, tm, lambda i: (i, 0))
    vec = pl.BlockSpec((1, n), lambda i: (0, 0))
    return pl.pallas_call(
        functools.partial(_ple_kernel, alpha=alpha),
        grid=(m // tm,),
        in_specs=[pl.BlockSpec((tm, k), lambda i: (i, 0)),
                  pl.BlockSpec((k, n), lambda i: (0, 0)),
                  pl.BlockSpec((tm, kp), lambda i: (i, 0)),
                  pl.BlockSpec((kp, n), lambda i: (0, 0)),
                  pl.BlockSpec((tm, n), lambda i: (i, 0)), vec, vec],
        out_specs=out_specs,
        out_shape=out_shapes,
        compiler_params=_params(("parallel",), blocks, 4 * _nbytes((tm, n), F32)),
        name="ple_ln",
    )(xb, wg, p, wp, x, g, b)


def kernel(x, p, positions, w_in, b_in, conv_w, conv_b, conv_ln_g, conv_ln_b,
           mconv_w, mconv_b, w_br_conv, w_br_attn, w_br_mlstm, w_out,
           ln_mix_g, ln_mix_b, w_ffn_gate, w_ffn_up, w_ffn_down, ln_ffn_g, ln_ffn_b,
           w_ple_gate, w_ple_proj, ln_ple_g, ln_ple_b):
    batch, seq, d = x.shape
    depth = w_in.shape[0]
    n = batch * seq
    conv_ch = conv_w.shape[2]
    attn_w = ATTN_HEADS * ATTN_HEAD_DIM
    mlstm_w = w_br_mlstm.shape[1]
    alpha = float((2 * depth) ** 0.25)
    assert conv_ch == attn_w == mlstm_w == d // 2

    off_va = 2 * conv_ch + 2 * attn_w
    off_qkm = off_va + attn_w
    off_vm = off_qkm + 2 * mlstm_w
    gate_lo = off_vm + mlstm_w
    gate_hi = gate_lo + 2 * MLSTM_HEADS
    col_val, col_gate, col_q, col_k = 0, 1, 2, 3
    col_qkm = 2
    col_v, col_vm, col_om, col_g0 = 6, 7, 8, 9

    half = ROPE_DIM // 2
    inv_freq = ROPE_THETA ** (-jnp.arange(half, dtype=F32) / half)
    inv_lane = jnp.concatenate(
        [inv_freq, inv_freq, jnp.zeros((LANES - ROPE_DIM,), F32)])[None, :]
    pos_f = positions.astype(F32).reshape(n, 1)
    c_tab, s_tab = _rope_tables(pos_f, inv_lane, 2048)

    xf = x.reshape(n, d)
    xb = xf.astype(BF16)
    for i in range(depth):
        def regroup(t):
            return jnp.concatenate(
                [t[..., :off_va], t[..., off_qkm:off_vm], t[..., off_va:off_qkm],
                 t[..., off_vm:gate_lo], t[..., gate_hi:]], axis=-1)

        w_main = regroup(w_in[i])
        b_main = regroup(b_in[i])[None, :]
        w_if = jnp.pad(w_in[i, :, gate_lo:gate_hi], ((0, 0), (0, LANES - 2 * MLSTM_HEADS)))
        b_if = jnp.pad(b_in[i, gate_lo:gate_hi], (0, LANES - 2 * MLSTM_HEADS))[None, :]

        z = _proj(xb, w_main.astype(BF16), b_main, F32, 1024, 1024)
        gates_col = _proj(xb, w_if.astype(BF16), b_if, F32, 1024, LANES)
        gates_row = gates_col[:, :2 * MLSTM_HEADS].reshape(batch, seq, 2 * MLSTM_HEADS)
        gates_row = gates_row.transpose(0, 2, 1)

        u_conv = _conv_module(z, col_val, col_gate, conv_w[i], conv_b[i][None, :],
                              conv_ln_g[i][None, :], conv_ln_b[i][None, :],
                              batch, seq, 256)

        q_a, k_a, v_a, kmean = _attn_prep(z, col_q, col_k, col_v, c_tab, s_tab, attn_w)
        kmean = kmean.reshape(batch, seq // MOBA_BLOCK, attn_w)
        o_attn = _moba(q_a, k_a, v_a, kmean, batch, seq)

        dh_m = mlstm_w // MLSTM_HEADS
        qk_m = _qk_conv(z, col_qkm, mconv_w[i], mconv_b[i][None, :], batch, seq, 256,
                        float(dh_m ** -0.5))
        h_m = _mlstm(qk_m, z, col_vm, col_om, gates_col, gates_row, batch, seq)

        merged = _merge(u_conv, o_attn, h_m, w_br_conv[i].astype(BF16),
                        w_br_attn[i].astype(BF16), w_br_mlstm[i].astype(BF16),
                        z, col_g0, 512, d // 2)
        xf, xb = _out_ln(merged, w_out[i].astype(BF16), xf, ln_mix_g[i][None, :],
                         ln_mix_b[i][None, :], alpha, 256)

        hid = _ffn_up(xb, w_ffn_gate[i].astype(BF16), w_ffn_up[i].astype(BF16), 1024, 512)
        xf, xb = _ffn_down(hid, w_ffn_down[i].astype(BF16), xf, ln_ffn_g[i][None, :],
                           ln_ffn_b[i][None, :], alpha, 512, 1408)

        xf, xb = _ple(xb, w_ple_gate[i].astype(BF16), p[i].reshape(n, -1),
                      w_ple_proj[i].astype(BF16), xf, ln_ple_g[i][None, :],
                      ln_ple_b[i][None, :], alpha, 256)
    return xf.reshape(batch, seq, d)
```

```python
import functools

import jax
import jax.numpy as jnp
import numpy as np
from jax import lax
from jax.experimental import pallas as pl
from jax.experimental.pallas import tpu as pltpu

F32 = jnp.float32
BF16 = jnp.bfloat16

LN_EPS = 1e-5
ATTN_HEADS = 8
ATTN_HEAD_DIM = 128
ROPE_DIM = ATTN_HEAD_DIM // 4
ROPE_THETA = 500000.0
MOBA_BLOCK = 256
MOBA_TOPK = 3
MLSTM_HEADS = 4
MLSTM_CHUNK = 256
LANES = 128
SUBLANES = 8
CONV_HALO = 32
MCONV_HALO = 8
CONV_ROWS = 128
VMEM_LIMIT_CAP = 58 * 1024 * 1024


def _nbytes(shape, dtype):
    return int(np.prod(shape)) * jnp.dtype(dtype).itemsize


def _params(semantics, pipelined_bytes, resident_bytes=0):
    need = 2 * pipelined_bytes + resident_bytes + (4 << 20)
    return pltpu.CompilerParams(
        dimension_semantics=semantics,
        vmem_limit_bytes=int(min(max(need, 16 << 20), VMEM_LIMIT_CAP)))


def _resident(shape, index_map):
    return pl.BlockSpec(shape, index_map, pipeline_mode=pl.Buffered(1))


def _row_subtiles(rows, sub):
    return [slice(r, r + sub) for r in range(0, rows, sub)]


def _sigmoid(x):
    return 0.5 * jnp.tanh(0.5 * x) + 0.5


def _layer_norm(y, g, b):
    mu = jnp.mean(y, axis=-1, keepdims=True)
    d = y - mu
    var = jnp.mean(d * d, axis=-1, keepdims=True)
    return d * lax.rsqrt(var + LN_EPS) * g + b


def _dot(a, b):
    return jnp.dot(a, b, preferred_element_type=F32)


def _dot_nt(a, b):
    return lax.dot_general(a, b, (((1,), (1,)), ((), ())), preferred_element_type=F32)


def _dot_tn(a, b):
    return lax.dot_general(a, b, (((0,), (0,)), ((), ())), preferred_element_type=F32)


def _split3(x):
    h1 = x.astype(BF16)
    r1 = x - h1.astype(F32)
    h2 = r1.astype(BF16)
    r2 = r1 - h2.astype(F32)
    return h1, h2, r2.astype(BF16)


def _proj_kernel(a_ref, w_ref, b_ref, o_ref, wb_sc):
    @pl.when(pl.program_id(1) == 0)
    def _():
        wb_sc[...] = w_ref[...].astype(BF16)

    acc = _dot(a_ref[...], wb_sc[...])
    o_ref[...] = (acc + b_ref[...]).astype(o_ref.dtype)


def _proj(a, w, bias, layer, n, out_dtype, tm, tn):
    m, k = a.shape
    blocks = (_nbytes((tm, k), a.dtype) + _nbytes((k, tn), w.dtype)
              + _nbytes((tm, tn), out_dtype) + _nbytes((1, tn), F32))
    return pl.pallas_call(
        _proj_kernel,
        grid=(n // tn, m // tm),
        in_specs=[pl.BlockSpec((tm, k), lambda j, i: (i, 0)),
                  pl.BlockSpec((None, k, tn), lambda j, i: (layer, 0, j)),
                  pl.BlockSpec((None, 1, tn), lambda j, i: (layer, 0, j))],
        out_specs=pl.BlockSpec((tm, tn), lambda j, i: (i, j)),
        out_shape=jax.ShapeDtypeStruct((m, n), out_dtype),
        scratch_shapes=[pltpu.VMEM((k, tn), BF16)],
        compiler_params=_params(("parallel", "arbitrary"), blocks,
                                _nbytes((k, tn), BF16) + _nbytes((tm, tn), F32)),
        name="in_proj",
    )(a, w, bias)


def _rope_table_kernel(pos_ref, inv_ref, c_ref, s_ref):
    ang = pos_ref[...] * inv_ref[...]
    lane = lax.broadcasted_iota(jnp.int32, ang.shape, 1)
    half = ROPE_DIM // 2
    c_ref[...] = jnp.where(lane < ROPE_DIM, jnp.cos(ang), 1.0)
    sn = jnp.sin(ang)
    s_ref[...] = jnp.where(lane < half, -sn, jnp.where(lane < ROPE_DIM, sn, 0.0))


def _rope_tables(pos_f, inv_lane, tm):
    n = pos_f.shape[0]
    return pl.pallas_call(
        _rope_table_kernel,
        grid=(n // tm,),
        in_specs=[pl.BlockSpec((tm, 1), lambda i: (i, 0)),
                  pl.BlockSpec((1, LANES), lambda i: (0, 0))],
        out_specs=[pl.BlockSpec((tm, LANES), lambda i: (i, 0)),
                   pl.BlockSpec((tm, LANES), lambda i: (i, 0))],
        out_shape=[jax.ShapeDtypeStruct((n, LANES), F32),
                   jax.ShapeDtypeStruct((n, LANES), F32)],
        compiler_params=_params(("parallel",), 3 * _nbytes((tm, LANES), F32)),
        name="rope_tables",
    )(pos_f, inv_lane)


def _attn_prep_kernel(zq_ref, zk_ref, zv_ref, c_ref, s_ref,
                      qt_ref, k_ref, vt_ref, km_ref):
    width = zq_ref.shape[1]
    reps = width // LANES
    half = ROPE_DIM // 2
    c = jnp.concatenate([c_ref[...]] * reps, axis=1)
    s = jnp.concatenate([s_ref[...]] * reps, axis=1)
    lane = lax.broadcasted_iota(jnp.int32, c.shape, 1) % LANES
    low = lane < half

    def rot(x):
        partner = jnp.where(low, pltpu.roll(x, width - half, 1), pltpu.roll(x, half, 1))
        return x * c + partner * s

    q = rot(zq_ref[...].astype(F32))
    k = rot(zk_ref[...].astype(F32))
    qt_ref[...] = (q * (ATTN_HEAD_DIM ** -0.5)).T
    k_ref[...] = k.astype(BF16)
    vt_ref[...] = zv_ref[...].astype(F32).T.astype(BF16)
    km_ref[...] = jnp.mean(k, axis=0, keepdims=True)


def _attn_prep(z, col_q, col_k, col_v, c_tab, s_tab, width):
    n = z.shape[0]
    blk = MOBA_BLOCK
    nblk = n // blk
    blocks = (3 * _nbytes((blk, width), z.dtype) + 2 * _nbytes((blk, LANES), F32)
              + _nbytes((blk, width), F32) + 2 * _nbytes((blk, width), BF16))
    return pl.pallas_call(
        _attn_prep_kernel,
        grid=(nblk,),
        in_specs=[pl.BlockSpec((blk, width), lambda i: (i, col_q)),
                  pl.BlockSpec((blk, width), lambda i: (i, col_k)),
                  pl.BlockSpec((blk, width), lambda i: (i, col_v)),
                  pl.BlockSpec((blk, LANES), lambda i: (i, 0)),
                  pl.BlockSpec((blk, LANES), lambda i: (i, 0))],
        out_specs=[pl.BlockSpec((width, blk), lambda i: (0, i)),
                   pl.BlockSpec((blk, width), lambda i: (i, 0)),
                   pl.BlockSpec((width, blk), lambda i: (0, i)),
                   pl.BlockSpec((None, 1, width), lambda i: (i, 0, 0))],
        out_shape=[jax.ShapeDtypeStruct((width, n), F32),
                   jax.ShapeDtypeStruct((n, width), BF16),
                   jax.ShapeDtypeStruct((width, n), BF16),
                   jax.ShapeDtypeStruct((nblk, 1, width), F32)],
        compiler_params=_params(("parallel",), blocks, 8 * _nbytes((blk, width), F32)),
        name="attn_prep",
    )(z, z, z, c_tab, s_tab)


def _moba_kernel(qt_ref, k_ref, vt_ref, km_ref, o_ref, s_sc, p_sc):
    blk = MOBA_BLOCK
    nb = km_ref.shape[0]
    km1, km2, km3 = _split3(km_ref[...])
    key_pos = lax.broadcasted_iota(jnp.int32, (blk, blk), 0)
    qry_pos = lax.broadcasted_iota(jnp.int32, (blk, blk), 1)
    causal_bias = jnp.where(key_pos <= qry_pos, 0.0, -jnp.inf)
    bid = lax.broadcasted_iota(jnp.int32, (nb, blk), 0)

    for i in range(nb):
        qs = slice(i * blk, (i + 1) * blk)
        qt = qt_ref[:, qs]
        qtb = qt.astype(BF16)

        if i > 0:
            q1, q2, q3 = _split3(qt)
            gate = (_dot(km1, q1) + (_dot(km1, q2) + _dot(km2, q1))
                    + (_dot(km1, q3) + _dot(km2, q2) + _dot(km3, q1)))
            rank = jnp.zeros(gate.shape, F32)
            for n in range(i):
                gn = gate[n:n + 1, :]
                beats = jnp.logical_or(gn > gate, jnp.logical_and(gn == gate, n < bid))
                rank = rank + jnp.where(beats, 1.0, 0.0)
            block_bias = jnp.where(rank < MOBA_TOPK, 0.0, -jnp.inf)

        m = None
        for j in range(i + 1):
            ks = slice(j * blk, (j + 1) * blk)
            s = _dot(k_ref[ks, :], qtb)
            s = s + (causal_bias if j == i else block_bias[j:j + 1, :])
            s_sc[ks, :] = s
            mj = jnp.max(s, axis=0, keepdims=True)
            m = mj if m is None else jnp.maximum(m, mj)

        l = jnp.zeros((1, blk), F32)
        for j in range(i + 1):
            ks = slice(j * blk, (j + 1) * blk)
            p = jnp.exp(s_sc[ks, :] - m)
            l = l + jnp.sum(p, axis=0, keepdims=True)
            p_sc[ks, :] = p.astype(BF16)

        nk = (i + 1) * blk
        ot = _dot(vt_ref[:, 0:nk], p_sc[0:nk, :])
        o_ref[qs, :] = (ot / l).T.astype(o_ref.dtype)


def _moba(qt, k, vt, kmean, batch, seq):
    n, width = k.shape
    heads = width // ATTN_HEAD_DIM
    blk = MOBA_BLOCK
    nb = seq // blk
    dh = ATTN_HEAD_DIM
    blocks = (_nbytes((dh, seq), F32) + 2 * _nbytes((seq, dh), BF16)
              + _nbytes((nb, dh), F32) + _nbytes((seq, dh), BF16))
    scratch = _nbytes((seq, blk), F32) + _nbytes((seq, blk), BF16)
    return pl.pallas_call(
        _moba_kernel,
        grid=(batch, heads),
        in_specs=[pl.BlockSpec((dh, seq), lambda b, h: (h, b)),
                  pl.BlockSpec((seq, dh), lambda b, h: (b, h)),
                  pl.BlockSpec((dh, seq), lambda b, h: (h, b)),
                  pl.BlockSpec((None, nb, dh), lambda b, h: (b, 0, h))],
        out_specs=pl.BlockSpec((seq, dh), lambda b, h: (b, h)),
        out_shape=jax.ShapeDtypeStruct((n, width), BF16),
        scratch_shapes=[pltpu.VMEM((seq, blk), F32), pltpu.VMEM((seq, blk), BF16)],
        compiler_params=_params(("parallel", "parallel"), blocks,
                                scratch + 8 * _nbytes((blk, blk), F32)),
        name="moba_attention",
    )(qt, k, vt, kmean)


def _conv_taps(xs_ref, w_ref, bias_ref, out_ref, halo):
    width = w_ref.shape[0]
    rows = out_ref.shape[0]
    base = halo - (width - 1)
    phases = [[j for j in range(width) if (base + j) % SUBLANES == ph]
              for ph in range(SUBLANES)]

    def column(cidx, carry):
        cols = pl.ds(pl.multiple_of(cidx * LANES, LANES), LANES)
        for r0 in range(0, rows, CONV_ROWS):
            acc = jnp.zeros((CONV_ROWS, LANES), F32) + bias_ref[:, cols]
            for taps in phases:
                if not taps:
                    continue
                span = taps[-1] - taps[0] + CONV_ROWS
                xp = xs_ref[pl.ds(r0 + base + taps[0], span), cols]
                part = None
                for j in taps:
                    d = j - taps[0]
                    term = xp[d:d + CONV_ROWS, :] * w_ref[j:j + 1, cols]
                    part = term if part is None else part + term
                acc = acc + part
            out_ref[r0:r0 + CONV_ROWS, cols] = acc
        return carry

    lax.fori_loop(0, out_ref.shape[1] // LANES, column, 0)


def _conv_module_kernel(val_ref, gate_ref, hval_ref, hgate_ref, w_ref, b_ref,
                        g_ref, beta_ref, o_ref, xs_sc, y_sc):
    halo = hval_ref.shape[0]
    has_past = pl.program_id(1) > 0
    hv = hval_ref[...].astype(F32)
    hg = hgate_ref[...].astype(F32)
    xs_sc[0:halo, :] = jnp.where(has_past, hv * _sigmoid(hg), 0.0)
    xs_sc[halo:, :] = val_ref[...].astype(F32) * _sigmoid(gate_ref[...].astype(F32))
    _conv_taps(xs_sc, w_ref, b_ref, y_sc, halo)
    u = _layer_norm(y_sc[...], g_ref[...], beta_ref[...])
    o_ref[...] = (u * _sigmoid(u)).astype(o_ref.dtype)


def _conv_module(z, col_val, col_gate, w, b, g, beta, batch, seq, ts):
    n = z.shape[0]
    ch = w.shape[1]
    halo = CONV_HALO
    nt = seq // ts
    hpb = ts // halo
    blocks = (2 * _nbytes((ts, ch), z.dtype) + 2 * _nbytes((halo, ch), z.dtype)
              + _nbytes(w.shape, F32) + 3 * _nbytes((1, ch), F32) + _nbytes((ts, ch), BF16))

    def cur(col):
        return pl.BlockSpec((ts, ch), lambda bi, i: (bi * nt + i, col))

    def past(col):
        return pl.BlockSpec(
            (halo, ch), lambda bi, i: (jnp.maximum((bi * nt + i) * hpb - 1, 0), col))

    vec = pl.BlockSpec((1, ch), lambda bi, i: (0, 0))
    return pl.pallas_call(
        _conv_module_kernel,
        grid=(batch, nt),
        in_specs=[cur(col_val), cur(col_gate), past(col_val), past(col_gate),
                  pl.BlockSpec(w.shape, lambda bi, i: (0, 0)), vec, vec, vec],
        out_specs=pl.BlockSpec((ts, ch), lambda bi, i: (bi * nt + i, 0)),
        out_shape=jax.ShapeDtypeStruct((n, ch), BF16),
        scratch_shapes=[pltpu.VMEM((halo + ts, ch), F32), pltpu.VMEM((ts, ch), F32)],
        compiler_params=_params(("parallel", "arbitrary"), blocks,
                                _nbytes((halo + 5 * ts, ch), F32)),
        name="conv_module",
    )(z, z, z, z, w, b, g, beta)


def _qk_conv_kernel(x_ref, hx_ref, w_ref, b_ref, o_ref, xs_sc, y_sc, *, k_scale):
    halo = hx_ref.shape[0]
    has_past = pl.program_id(1) > 0
    xs_sc[0:halo, :] = jnp.where(has_past, hx_ref[...].astype(F32), 0.0)
    xs_sc[halo:, :] = x_ref[...].astype(F32)
    _conv_taps(xs_sc, w_ref, b_ref, y_sc, halo)
    y = y_sc[...]
    scale = jnp.where(pl.program_id(2) == 1, k_scale, 1.0)
    o_ref[...] = (y * _sigmoid(y) * scale).astype(o_ref.dtype)


def _qk_conv(z, col0, w, b, batch, seq, ts, k_scale):
    n = z.shape[0]
    ch = w.shape[1] // 2
    halo = MCONV_HALO
    nt = seq // ts
    hpb = ts // halo
    blocks = (_nbytes((ts, ch), z.dtype) + _nbytes((halo, ch), z.dtype)
              + _nbytes((w.shape[0], ch), F32) + _nbytes((1, ch), F32)
              + _nbytes((ts, ch), BF16))
    return pl.pallas_call(
        functools.partial(_qk_conv_kernel, k_scale=k_scale),
        grid=(batch, nt, 2),
        in_specs=[pl.BlockSpec((ts, ch), lambda bi, i, c: (bi * nt + i, col0 + c)),
                  pl.BlockSpec((halo, ch),
                               lambda bi, i, c: (jnp.maximum((bi * nt + i) * hpb - 1, 0),
                                                 col0 + c)),
                  pl.BlockSpec((w.shape[0], ch), lambda bi, i, c: (0, c)),
                  pl.BlockSpec((1, ch), lambda bi, i, c: (0, c))],
        out_specs=pl.BlockSpec((ts, ch), lambda bi, i, c: (bi * nt + i, c)),
        out_shape=jax.ShapeDtypeStruct((n, 2 * ch), BF16),
        scratch_shapes=[pltpu.VMEM((halo + ts, ch), F32), pltpu.VMEM((ts, ch), F32)],
        compiler_params=_params(("parallel", "arbitrary", "arbitrary"), blocks,
                                _nbytes((halo + 4 * ts, ch), F32)),
        name="mlstm_qk_conv",
    )(z, z, w, b)


def _log_sigmoid(x):
    return jnp.minimum(x, 0.0) - jnp.log(1.0 + jnp.exp(-jnp.abs(x)))


def _mlstm_kernel(qk_ref, v_ref, og_ref, gcol_ref, grow_ref, o_ref,
                  c_sc, n_sc, m_sc):
    chunk = v_ref.shape[0]
    heads = MLSTM_HEADS
    dh = v_ref.shape[1] // heads
    width = v_ref.shape[1]

    @pl.when(pl.program_id(1) == 0)
    def _():
        c_sc[...] = jnp.zeros_like(c_sc)
        n_sc[...] = jnp.zeros_like(n_sc)
        m_sc[...] = jnp.zeros_like(m_sc)

    row = lax.broadcasted_iota(jnp.int32, (chunk, chunk), 0)
    col = lax.broadcasted_iota(jnp.int32, (chunk, chunk), 1)
    causal = col <= row
    tri = jnp.where(causal, 1.0, 0.0).astype(BF16)
    tri_t = jnp.where(row <= col, 1.0, 0.0).astype(BF16)

    gcol = gcol_ref[...]
    grow = grow_ref[...]
    lf_col = _log_sigmoid(gcol)
    lf_row = _log_sigmoid(grow)
    c1, c2, c3 = _split3(lf_col)
    bcol = _dot(tri, c1) + _dot(tri, c2) + _dot(tri, c3)
    r1, r2, r3 = _split3(lf_row)
    brow = _dot(r1, tri_t) + _dot(r2, tri_t) + _dot(r3, tri_t)

    for h in range(heads):
        hs = slice(h * dh, (h + 1) * dh)
        ic_col = gcol[:, h:h + 1]
        b_col = bcol[:, heads + h:heads + h + 1]
        ic_row = grow[h:h + 1, :]
        b_row = brow[heads + h:heads + h + 1, :]
        m = m_sc[h:h + 1, 0:1]
        q = qk_ref[:, h * dh:(h + 1) * dh]
        k = qk_ref[:, width + h * dh:width + (h + 1) * dh]
        v = v_ref[:, hs].astype(BF16)
        c_old = c_sc[h]
        n_old = n_sc[h]

        log_d = jnp.where(causal, b_col + (ic_row - b_row), -jnp.inf)
        log_inter = b_col + m
        m_t = jnp.maximum(log_inter, jnp.max(log_d, axis=1, keepdims=True))
        d = jnp.exp(log_d - m_t)
        inter = jnp.exp(log_inter - m_t)
        s = _dot_nt(q, k) * d
        num = _dot(s.astype(BF16), v) + inter * _dot(q, c_old.astype(BF16))
        qn = jnp.sum(q.astype(F32) * n_old, axis=1, keepdims=True)
        den = jnp.sum(s, axis=1, keepdims=True) + inter * qn
        hid = num / jnp.maximum(jnp.abs(den), jnp.exp(-m_t))
        og = og_ref[:, hs].astype(F32)
        o_ref[:, hs] = (hid * _sigmoid(og)).astype(o_ref.dtype)

        b_last = b_col[chunk - 1:chunk, :]
        log_w = b_last - b_col + ic_col
        m_new = jnp.maximum(b_last + m, jnp.max(log_w, axis=0, keepdims=True))
        w = jnp.exp(log_w - m_new)
        decay = jnp.exp(b_last + m - m_new)
        kw = k.astype(F32) * w
        c_sc[h] = decay * c_old + _dot_tn(kw.astype(BF16), v)
        n_sc[h] = decay * n_old + jnp.sum(kw, axis=0, keepdims=True)
        m_sc[h:h + 1, :] = jnp.broadcast_to(m_new, (1, m_sc.shape[1]))


def _mlstm(qk, zv, col_v, zo, col_o, gates_col, gates_row, batch, seq):
    n = qk.shape[0]
    width = qk.shape[1] // 2
    chunk = MLSTM_CHUNK
    nc = seq // chunk
    dh = width // MLSTM_HEADS
    blocks = (_nbytes((chunk, 2 * width), qk.dtype) + _nbytes((chunk, width), zv.dtype)
              + _nbytes((chunk, width), zo.dtype)
              + _nbytes((chunk, LANES), F32) + _nbytes((8, chunk), F32)
              + _nbytes((chunk, width), BF16))
    scratch = _nbytes((MLSTM_HEADS, dh, dh), F32)
    return pl.pallas_call(
        _mlstm_kernel,
        grid=(batch, nc),
        in_specs=[pl.BlockSpec((chunk, 2 * width), lambda b, c: (b * nc + c, 0)),
                  pl.BlockSpec((chunk, width), lambda b, c: (b * nc + c, col_v)),
                  pl.BlockSpec((chunk, width), lambda b, c: (b * nc + c, col_o)),
                  pl.BlockSpec((chunk, LANES), lambda b, c: (b * nc + c, 0)),
                  pl.BlockSpec((None, 8, chunk), lambda b, c: (b, 0, c))],
        out_specs=pl.BlockSpec((chunk, width), lambda b, c: (b * nc + c, 0)),
        out_shape=jax.ShapeDtypeStruct((n, width), BF16),
        scratch_shapes=[pltpu.VMEM((MLSTM_HEADS, dh, dh), F32),
                        pltpu.VMEM((MLSTM_HEADS, 1, dh), F32),
                        pltpu.VMEM((8, LANES), F32)],
        compiler_params=_params(("parallel", "arbitrary"), blocks,
                                scratch + 16 * _nbytes((chunk, chunk), F32)),
        name="mlstm_scan",
    )(qk, zv, zo, gates_col, gates_row)


def _merge_kernel(uc_ref, ua_ref, um_ref, wc_ref, wa_ref, wm_ref,
                  g0_ref, g1_ref, g2_ref, o_ref, *, sub):
    for rows in _row_subtiles(o_ref.shape[0], sub):
        acc = _sigmoid(g0_ref[rows, :].astype(F32)) * _dot(uc_ref[rows, :], wc_ref[...])
        acc = acc + _sigmoid(g1_ref[rows, :].astype(F32)) * _dot(ua_ref[rows, :], wa_ref[...])
        acc = acc + _sigmoid(g2_ref[rows, :].astype(F32)) * _dot(um_ref[rows, :], wm_ref[...])
        o_ref[rows, :] = acc.astype(o_ref.dtype)


def _merge(uc, ua, um, wc, wa, wm, z, gate_col0, tm, tn, sub):
    m, k = uc.shape
    n = wc.shape[1]
    nj = n // tn
    blocks = (3 * _nbytes((tm, k), BF16) + 3 * _nbytes((k, tn), BF16)
              + 3 * _nbytes((tm, tn), z.dtype) + _nbytes((tm, tn), BF16))
    a_spec = pl.BlockSpec((tm, k), lambda i, j: (i, 0))
    w_spec = pl.BlockSpec((k, tn), lambda i, j: (0, j))

    def gate_spec(branch):
        return pl.BlockSpec((tm, tn), lambda i, j: (i, gate_col0 + branch * nj + j))

    return pl.pallas_call(
        functools.partial(_merge_kernel, sub=sub),
        grid=(m // tm, nj),
        in_specs=[a_spec, a_spec, a_spec, w_spec, w_spec, w_spec,
                  gate_spec(0), gate_spec(1), gate_spec(2)],
        out_specs=pl.BlockSpec((tm, tn), lambda i, j: (i, j)),
        out_shape=jax.ShapeDtypeStruct((m, n), BF16),
        compiler_params=_params(("parallel", "arbitrary"), blocks,
                                6 * _nbytes((sub, tn), F32)),
        name="branch_merge",
    )(uc, ua, um, wc, wa, wm, z, z, z)


def _write_ln(y, g_ref, b_ref, xo_ref, xb_ref, rows):
    out = _layer_norm(y, g_ref[...], b_ref[...])
    xo_ref[rows, :] = out
    xb_ref[rows, :] = out.astype(BF16)


def _proj_ln_kernel(a_ref, w_ref, x_ref, g_ref, b_ref, xo_ref, xb_ref, *, alpha, sub):
    for rows in _row_subtiles(xo_ref.shape[0], sub):
        y = alpha * x_ref[rows, :] + _dot(a_ref[rows, :], w_ref[...])
        _write_ln(y, g_ref, b_ref, xo_ref, xb_ref, rows)


def _row_outs(m, n, tm):
    specs = [pl.BlockSpec((tm, n), lambda i: (i, 0)), pl.BlockSpec((tm, n), lambda i: (i, 0))]
    shapes = [jax.ShapeDtypeStruct((m, n), F32), jax.ShapeDtypeStruct((m, n), BF16)]
    return specs, shapes


def _proj_ln(a, w, x, g, b, alpha, tm, sub, name):
    m, k = a.shape
    n = w.shape[1]
    blocks = (_nbytes((tm, k), BF16) + 2 * _nbytes((tm, n), F32) + _nbytes((tm, n), BF16))
    resident = _nbytes((k, n), BF16) + 2 * _nbytes((1, n), F32)
    out_specs, out_shapes = _row_outs(m, n, tm)
    vec = _resident((1, n), lambda i: (0, 0))
    return pl.pallas_call(
        functools.partial(_proj_ln_kernel, alpha=alpha, sub=sub),
        grid=(m // tm,),
        in_specs=[pl.BlockSpec((tm, k), lambda i: (i, 0)),
                  _resident((k, n), lambda i: (0, 0)),
                  pl.BlockSpec((tm, n), lambda i: (i, 0)), vec, vec],
        out_specs=out_specs,
        out_shape=out_shapes,
        compiler_params=_params(("parallel",), blocks,
                                resident + 4 * _nbytes((sub, n), F32)),
        name=name,
    )(a, w, x, g, b)


def _ffn_up_kernel(a_ref, wg_ref, wu_ref, o_ref, wg_sc, wu_sc):
    @pl.when(pl.program_id(1) == 0)
    def _():
        wg_sc[...] = wg_ref[...].astype(BF16)
        wu_sc[...] = wu_ref[...].astype(BF16)

    a = a_ref[...]
    gate = _dot(a, wg_sc[...])
    up = _dot(a, wu_sc[...])
    o_ref[...] = (gate * _sigmoid(gate) * up).astype(o_ref.dtype)


def _ffn_up(a, wg, wu, layer, tm, tn):
    m, k = a.shape
    n = wg.shape[2]
    blocks = (_nbytes((tm, k), BF16) + 2 * _nbytes((k, tn), wg.dtype)
              + _nbytes((tm, tn), BF16))
    w_spec = pl.BlockSpec((None, k, tn), lambda j, i: (layer, 0, j))
    return pl.pallas_call(
        _ffn_up_kernel,
        grid=(n // tn, m // tm),
        in_specs=[pl.BlockSpec((tm, k), lambda j, i: (i, 0)), w_spec, w_spec],
        out_specs=pl.BlockSpec((tm, tn), lambda j, i: (i, j)),
        out_shape=jax.ShapeDtypeStruct((m, n), BF16),
        scratch_shapes=[pltpu.VMEM((k, tn), BF16), pltpu.VMEM((k, tn), BF16)],
        compiler_params=_params(("parallel", "arbitrary"), blocks,
                                2 * _nbytes((k, tn), BF16) + 4 * _nbytes((tm, tn), F32)),
        name="ffn_up",
    )(a, wg, wu)


def _ple_kernel(xb_ref, wg_ref, p_ref, wp_ref, x_ref, g_ref, b_ref, xo_ref, xb_out_ref,
                *, alpha, sub):
    for rows in _row_subtiles(xo_ref.shape[0], sub):
        gate = _sigmoid(_dot(xb_ref[rows, :], wg_ref[...]))
        proj = _dot(p_ref[rows, :].astype(BF16), wp_ref[...])
        y = alpha * x_ref[rows, :] + gate * proj
        _write_ln(y, g_ref, b_ref, xo_ref, xb_out_ref, rows)


def _ple(xb, wg, p, wp, x, g, b, alpha, tm, sub):
    m, k = xb.shape
    n = wg.shape[1]
    kp = p.shape[1]
    blocks = (_nbytes((tm, k), BF16) + _nbytes((tm, kp), F32)
              + 2 * _nbytes((tm, n), F32) + _nbytes((tm, n), BF16))
    resident = _nbytes((k, n), BF16) + _nbytes((kp, n), BF16) + 2 * _nbytes((1, n), F32)
    out_specs, out_shapes = _row_outs(m, n, tm)
    vec = _resident((1, n), lambda i: (0, 0))
    return pl.pallas_call(
        functools.partial(_ple_kernel, alpha=alpha, sub=sub),
        grid=(m // tm,),
        in_specs=[pl.BlockSpec((tm, k), lambda i: (i, 0)),
                  _resident((k, n), lambda i: (0, 0)),
                  pl.BlockSpec((tm, kp), lambda i: (i, 0)),
                  _resident((kp, n), lambda i: (0, 0)),
                  pl.BlockSpec((tm, n), lambda i: (i, 0)), vec, vec],
        out_specs=out_specs,
        out_shape=out_shapes,
        compiler_params=_params(("parallel",), blocks,
                                resident + 5 * _nbytes((sub, n), F32)),
        name="ple_ln",
    )(xb, wg, p, wp, x, g, b)


def kernel(x, p, positions, w_in, b_in, conv_w, conv_b, conv_ln_g, conv_ln_b,
           mconv_w, mconv_b, w_br_conv, w_br_attn, w_br_mlstm, w_out,
           ln_mix_g, ln_mix_b, w_ffn_gate, w_ffn_up, w_ffn_down, ln_ffn_g, ln_ffn_b,
           w_ple_gate, w_ple_proj, ln_ple_g, ln_ple_b):
    batch, seq, d = x.shape
    depth = w_in.shape[0]
    n = batch * seq
    unit = conv_w.shape[2]
    attn_w = ATTN_HEADS * ATTN_HEAD_DIM
    mlstm_w = w_br_mlstm.shape[1]
    alpha = float((2 * depth) ** 0.25)
    assert unit == attn_w == mlstm_w == d // 2

    gate_lo = 8 * unit
    gate_hi = gate_lo + 2 * MLSTM_HEADS
    n_tail = w_in.shape[2] - gate_hi
    assert n_tail == 7 * unit
    col_val, col_gate, col_q, col_k, col_v, col_qkm, col_vm = 0, 1, 2, 3, 4, 5, 7
    col_om, col_g0 = 0, 1

    half = ROPE_DIM // 2
    inv_freq = ROPE_THETA ** (-jnp.arange(half, dtype=F32) / half)
    inv_lane = jnp.concatenate(
        [inv_freq, inv_freq, jnp.zeros((LANES - ROPE_DIM,), F32)])[None, :]
    pos_f = positions.astype(F32).reshape(n, 1)
    c_tab, s_tab = _rope_tables(pos_f, inv_lane, 2048)

    b_in3 = b_in[:, None, :]
    w_tail = w_in[:, :, gate_hi:].astype(BF16)
    b_tail = b_in3[:, :, gate_hi:]
    pad = LANES - 2 * MLSTM_HEADS
    w_if = jnp.pad(w_in[:, :, gate_lo:gate_hi], ((0, 0), (0, 0), (0, pad))).astype(BF16)
    b_if = jnp.pad(b_in3[:, :, gate_lo:gate_hi], ((0, 0), (0, 0), (0, pad)))

    xf = x.reshape(n, d)
    xb = xf.astype(BF16)
    for i in range(depth):
        z_head = _proj(xb, w_in, b_in3, i, gate_lo, BF16, 1024, unit)
        z_tail = _proj(xb, w_tail, b_tail, i, n_tail, BF16, 1024, unit)
        gates_col = _proj(xb, w_if, b_if, i, LANES, F32, 1024, LANES)
        gates_row = gates_col[:, :2 * MLSTM_HEADS].reshape(batch, seq, 2 * MLSTM_HEADS)
        gates_row = gates_row.transpose(0, 2, 1)

        u_conv = _conv_module(z_head, col_val, col_gate, conv_w[i], conv_b[i][None, :],
                              conv_ln_g[i][None, :], conv_ln_b[i][None, :],
                              batch, seq, 256)

        qt_a, k_a, vt_a, kmean = _attn_prep(z_head, col_q, col_k, col_v, c_tab, s_tab,
                                            attn_w)
        kmean = kmean.reshape(batch, seq // MOBA_BLOCK, attn_w)
        o_attn = _moba(qt_a, k_a, vt_a, kmean, batch, seq)

        dh_m = mlstm_w // MLSTM_HEADS
        qk_m = _qk_conv(z_head, col_qkm, mconv_w[i], mconv_b[i][None, :], batch, seq, 256,
                        float(dh_m ** -0.5))
        h_m = _mlstm(qk_m, z_head, col_vm, z_tail, col_om, gates_col, gates_row, batch, seq)

        merged = _merge(u_conv, o_attn, h_m, w_br_conv[i].astype(BF16),
                        w_br_attn[i].astype(BF16), w_br_mlstm[i].astype(BF16),
                        z_tail, col_g0, 512, unit, 256)
        xf, xb = _proj_ln(merged, w_out[i].astype(BF16), xf, ln_mix_g[i][None, :],
                          ln_mix_b[i][None, :], alpha, 512, 256, "out_proj_ln")

        hid = _ffn_up(xb, w_ffn_gate, w_ffn_up, i, 1024, 512)
        xf, xb = _proj_ln(hid, w_ffn_down[i].astype(BF16), xf, ln_ffn_g[i][None, :],
                          ln_ffn_b[i][None, :], alpha, 256, 128, "ffn_down_ln")

        xf, xb = _ple(xb, w_ple_gate[i].astype(BF16), p[i].reshape(n, -1),
                      w_ple_proj[i].astype(BF16), xf, ln_ple_g[i][None, :],
                      ln_ple_b[i][None, :], alpha, 512, 256)
    return xf.reshape(batch, seq, d)
```
